```python
import math
import jax, jax.numpy as jnp
from jax import lax
import numpy as np

D_MODEL = 1024
BATCH = 8
SEQ = 4096
DEPTH = 4

CONV_A_WIDTH = 512
CONV_A_KERNEL = 31
CONV_B_WIDTH = 512
CONV_B_KERNEL = 3
N_HEADS = 8
HEAD_DIM = 64
V_HEAD_DIM = 2 * HEAD_DIM
ATTN_QK_WIDTH = N_HEADS * 2 * HEAD_DIM
ATTN_V_WIDTH = N_HEADS * V_HEAD_DIM
Q_BLOCK = 128
ROPE_THETA = 10000.0
N_BRANCHES = 3
D_FF = 4 * D_MODEL
N_ADA = 6
NORM_EPS = 1e-6

IN_SIZES = (2 * CONV_A_WIDTH,
            3 * CONV_B_WIDTH,
            ATTN_QK_WIDTH,
            ATTN_QK_WIDTH,
            ATTN_V_WIDTH,
            N_BRANCHES * D_MODEL)
D_IN_PROJ = sum(IN_SIZES)
IN_SPLITS = tuple(int(s) for s in np.cumsum(IN_SIZES)[:-1])

kernel_name = "hybrid_conv_shortconv_diffattn_block"


def rmsnorm(x, g):
    xf = x.astype(jnp.float32)
    y = xf * lax.rsqrt(jnp.mean(xf * xf, axis=-1, keepdims=True) + NORM_EPS)
    return (y * g.astype(jnp.float32)).astype(x.dtype)


def layernorm(x, g, b):
    xf = x.astype(jnp.float32)
    mu = jnp.mean(xf, axis=-1, keepdims=True)
    var = jnp.mean(jnp.square(xf - mu), axis=-1, keepdims=True)
    y = (xf - mu) * lax.rsqrt(var + NORM_EPS)
    return (y * g.astype(jnp.float32) + b.astype(jnp.float32)).astype(x.dtype)


def modulate(h, shift, scale):
    return h * (1.0 + scale[:, None, :]) + shift[:, None, :]


def causal_depthwise_conv(x, w):
    k = w.shape[0]
    return lax.conv_general_dilated(
        x, w[:, None, :].astype(x.dtype), window_strides=(1,), padding=[(k - 1, 0)],
        dimension_numbers=('NWC', 'WIO', 'NWC'), feature_group_count=x.shape[-1])


def rope_tables(positions, dtype):
    inv_freq = ROPE_THETA ** (-jnp.arange(0, HEAD_DIM, 2, dtype=jnp.float32) / HEAD_DIM)
    ang = positions.astype(jnp.float32)[..., None] * inv_freq
    cos = jnp.cos(ang)[:, :, None, None, :].astype(dtype)
    sin = jnp.sin(ang)[:, :, None, None, :].astype(dtype)
    return cos, sin


def apply_rope(t, cos, sin):
    t1, t2 = jnp.split(t, 2, axis=-1)
    return jnp.concatenate([t1 * cos - t2 * sin, t2 * cos + t1 * sin], axis=-1)


def diff_attention(q, k, v, lam, subln_g, lambda_init):
    bsz, seq = q.shape[0], q.shape[1]
    scale = HEAD_DIM ** -0.5
    outs = []
    for i in range(seq // Q_BLOCK):
        lo, hi = i * Q_BLOCK, (i + 1) * Q_BLOCK
        qb, kb, vb = q[:, lo:hi], k[:, :hi], v[:, :hi]
        s = jnp.einsum('bqhcd,bkhcd->bhcqk', qb, kb).astype(jnp.float32) * scale
        mask = (lo + jnp.arange(Q_BLOCK))[:, None] >= jnp.arange(hi)[None, :]
        p = jax.nn.softmax(jnp.where(mask, s, -jnp.inf), axis=-1)
        a = p[:, :, 0] - lam * p[:, :, 1]
        outs.append(jnp.einsum('bhqk,bkhd->bqhd', a.astype(vb.dtype), vb))
    o = jnp.concatenate(outs, axis=1)
    o = rmsnorm(o, subln_g) * (1.0 - lambda_init)
    return o.reshape(bsz, seq, N_HEADS * V_HEAD_DIM)


def setup_inputs(seed: int = 0) -> dict:
    key = jax.random.key(seed)
    ks = iter(jax.random.split(key, 32))
    f32 = jnp.float32

    def nrm(shape, scale):
        return jax.random.normal(next(ks), shape, f32) * scale

    def gain(shape):
        return 1.0 + nrm(shape, 0.02)

    L, D = DEPTH, D_MODEL
    x = jax.random.normal(next(ks), (BATCH, SEQ, D), f32)
    c = jax.random.normal(next(ks), (BATCH, D), f32)
    offset = jax.random.randint(next(ks), (BATCH, 1), 0, SEQ, dtype=jnp.int32)
    positions = (offset + jnp.arange(SEQ, dtype=jnp.int32)[None, :]).astype(jnp.int32)
    return {
        "x": x,
        "c": c,
        "positions": positions,
        "w_ada": nrm((L, D, N_ADA * D), 0.5 * D ** -0.5),
        "b_ada": nrm((L, N_ADA * D), 0.02),
        "norm_mix_g": gain((L, D)),
        "w_in": nrm((L, D, D_IN_PROJ), D ** -0.5),
        "conv_a_w": nrm((L, CONV_A_KERNEL, CONV_A_WIDTH), CONV_A_KERNEL ** -0.5),
        "conv_a_b": nrm((L, CONV_A_WIDTH), 0.02),
        "ln_a_g": gain((L, CONV_A_WIDTH)),
        "ln_a_b": nrm((L, CONV_A_WIDTH), 0.02),
        "w_a_out": nrm((L, CONV_A_WIDTH, D), CONV_A_WIDTH ** -0.5),
        "conv_b_w": nrm((L, CONV_B_KERNEL, CONV_B_WIDTH), CONV_B_KERNEL ** -0.5),
        "w_b_out": nrm((L, CONV_B_WIDTH, D), CONV_B_WIDTH ** -0.5),
        "lam_q1": nrm((L, HEAD_DIM), 0.1),
        "lam_k1": nrm((L, HEAD_DIM), 0.1),
        "lam_q2": nrm((L, HEAD_DIM), 0.1),
        "lam_k2": nrm((L, HEAD_DIM), 0.1),
        "subln_g": gain((L, V_HEAD_DIM)),
        "w_c_out": nrm((L, ATTN_V_WIDTH, D), ATTN_V_WIDTH ** -0.5),
        "w_out": nrm((L, D, D), D ** -0.5),
        "norm_mlp_g": gain((L, D)),
        "w_ff1": nrm((L, D, D_FF), D ** -0.5),
        "w_ff2": nrm((L, D_FF, D), D_FF ** -0.5),
        "final_g": gain((D,)),
    }


def reference(x, c, positions, w_ada, b_ada, norm_mix_g, w_in, conv_a_w, conv_a_b,
              ln_a_g, ln_a_b, w_a_out, conv_b_w, w_b_out, lam_q1, lam_k1, lam_q2,
              lam_k2, subln_g, w_c_out, w_out, norm_mlp_g, w_ff1, w_ff2, final_g):
    bsz, seq, _ = x.shape
    cos, sin = rope_tables(positions, x.dtype)
    c_act = jax.nn.silu(c)
    for l in range(DEPTH):
        lambda_init = 0.8 - 0.6 * math.exp(-0.3 * l)
        ada = jnp.einsum('bd,de->be', c_act, w_ada[l]) + b_ada[l]
        sh_m, sc_m, g_m, sh_f, sc_f, g_f = jnp.split(ada, N_ADA, axis=-1)

        h = modulate(rmsnorm(x, norm_mix_g[l]), sh_m, sc_m)
        proj = jnp.einsum('bsd,de->bse', h, w_in[l])
        a_in, b_in, q, k, v, gate_pre = jnp.split(proj, IN_SPLITS, axis=-1)

        a = jax.nn.glu(a_in, axis=-1)
        a = causal_depthwise_conv(a, conv_a_w[l]) + conv_a_b[l]
        a = jax.nn.silu(layernorm(a, ln_a_g[l], ln_a_b[l]))
        y_a = jnp.einsum('bsc,cd->bsd', a, w_a_out[l])

        bg, cg, xb = jnp.split(b_in, 3, axis=-1)
        y_b = jnp.einsum('bsc,cd->bsd', bg * causal_depthwise_conv(cg * xb, conv_b_w[l]), w_b_out[l])

        q = apply_rope(q.reshape(bsz, seq, N_HEADS, 2, HEAD_DIM), cos, sin)
        k = apply_rope(k.reshape(bsz, seq, N_HEADS, 2, HEAD_DIM), cos, sin)
        v = v.reshape(bsz, seq, N_HEADS, V_HEAD_DIM)
        lam = (jnp.exp(jnp.sum(lam_q1[l].astype(jnp.float32) * lam_k1[l].astype(jnp.float32)))
               - jnp.exp(jnp.sum(lam_q2[l].astype(jnp.float32) * lam_k2[l].astype(jnp.float32)))
               + lambda_init)
        o = diff_attention(q, k, v, lam, subln_g[l], lambda_init)
        y_c = jnp.einsum('bsc,cd->bsd', o, w_c_out[l])

        g_a, g_b, g_c = jnp.split(jax.nn.sigmoid(gate_pre), N_BRANCHES, axis=-1)
        merged = g_a * y_a + g_b * y_b + g_c * y_c
        x = x + g_m[:, None, :] * jnp.einsum('bsd,de->bse', merged, w_out[l])

        h = modulate(rmsnorm(x, norm_mlp_g[l]), sh_f, sc_f)
        f = jnp.square(jax.nn.relu(jnp.einsum('bsd,df->bsf', h, w_ff1[l])))
        x = x + g_f[:, None, :] * jnp.einsum('bsf,fd->bsd', f, w_ff2[l])
    return rmsnorm(x, final_g)
```

```python
import functools
import math

import jax
import jax.numpy as jnp
from jax import lax
from jax.experimental import pallas as pl
from jax.experimental.pallas import tpu as pltpu

N_HEADS = 8
HEAD_DIM = 64
V_HEAD_DIM = 2 * HEAD_DIM
N_ADA = 6
N_BRANCHES = 3
NORM_EPS = 1e-6
ROPE_THETA = 10000.0

LANES = 128
VMEM_LIMIT_BYTES = 56 * 1024 * 1024

F32 = jnp.float32
BF16 = jnp.bfloat16


def _params(n_axes):
    return pltpu.CompilerParams(
        dimension_semantics=("arbitrary",) * n_axes,
        vmem_limit_bytes=VMEM_LIMIT_BYTES)


def _resident(shape):
    nd = len(shape)
    return pl.BlockSpec(shape, lambda *_: (0,) * nd, pipeline_mode=pl.Buffered(1))


def _dot(a, b):
    return jnp.dot(a, b, preferred_element_type=F32)


def _ada_kernel(c_ref, w_ref, b_ref, o_ref):
    c = c_ref[...]
    c_act = c * jax.nn.sigmoid(c)
    o_ref[...] = _dot(c_act.astype(BF16), w_ref[...].astype(BF16)) + b_ref[...]


def _ada_all_layers(c, w_ada, b_ada):
    n_layers, d, n_out = w_ada.shape
    bsz = c.shape[0]
    tn = n_out // 4
    return pl.pallas_call(
        _ada_kernel,
        grid=(n_layers, n_out // tn),
        in_specs=[
            pl.BlockSpec((bsz, d), lambda l, j: (0, 0)),
            pl.BlockSpec((None, d, tn), lambda l, j: (l, 0, j)),
            pl.BlockSpec((None, 1, tn), lambda l, j: (l, 0, j)),
        ],
        out_specs=pl.BlockSpec((None, bsz, tn), lambda l, j: (l, 0, j)),
        out_shape=jax.ShapeDtypeStruct((n_layers, bsz, n_out), F32),
        compiler_params=_params(2),
        name="ada",
    )(c, w_ada, b_ada.reshape(n_layers, 1, n_out))


def _rope_kernel(pos_ref, invf_ref, cos_ref, sin_ref):
    ang = pos_ref[...].astype(F32) * invf_ref[...]
    lane = lax.broadcasted_iota(jnp.int32, ang.shape, 1)
    sign = jnp.where((lane % HEAD_DIM) < HEAD_DIM // 2, -1.0, 1.0)
    cos_ref[...] = jnp.cos(ang)
    sin_ref[...] = jnp.sin(ang) * sign


def _rope_tables(positions):
    bsz, seq = positions.shape
    inv_freq = ROPE_THETA ** (-jnp.arange(0, HEAD_DIM, 2, dtype=F32) / HEAD_DIM)
    invf = jnp.tile(inv_freq, LANES // (HEAD_DIM // 2)).reshape(1, LANES)
    ts = min(seq, 2048)
    table = jax.ShapeDtypeStruct((bsz, seq, LANES), F32)
    return pl.pallas_call(
        _rope_kernel,
        grid=(bsz, seq // ts),
        in_specs=[
            pl.BlockSpec((None, ts, 1), lambda b, i: (b, i, 0)),
            pl.BlockSpec((1, LANES), lambda b, i: (0, 0)),
        ],
        out_specs=[pl.BlockSpec((None, ts, LANES), lambda b, i: (b, i, 0))] * 2,
        out_shape=[table, table],
        compiler_params=_params(2),
        name="rope_tables",
    )(positions.reshape(bsz, seq, 1), invf)


def _rms_modulate(x, g, shift, scale):
    ms = jnp.mean(x * x, axis=-1, keepdims=True)
    y = x * lax.rsqrt(ms + NORM_EPS) * g
    return y * (1.0 + scale) + shift


def _rope(t, cos, sin_signed, first_half):
    swapped = jnp.where(first_half, pltpu.roll(t, LANES - HEAD_DIM // 2, 1),
                        pltpu.roll(t, HEAD_DIM // 2, 1))
    return t * cos + swapped * sin_signed


def _inproj_kernel(x_ref, ada_ref, g_ref, w_ref, cos_ref, sin_ref,
                   a_ref, bg_ref, u_ref, q_ref, k_ref, v_ref, gate_ref,
                   *, ca, cb, dqk, dv, dg):
    ada = ada_ref[...]
    h = _rms_modulate(x_ref[...], g_ref[...], ada[0:1], ada[1:2]).astype(BF16)

    off = 0
    r = _dot(h, w_ref[:, off:off + 2 * ca])
    a_ref[...] = (r[:, :ca] * jax.nn.sigmoid(r[:, ca:])).astype(BF16)
    off += 2 * ca

    r = _dot(h, w_ref[:, off:off + 3 * cb])
    bg_ref[...] = r[:, :cb].astype(BF16)
    u_ref[...] = (r[:, cb:2 * cb] * r[:, 2 * cb:]).astype(BF16)
    off += 3 * cb

    cos = cos_ref[...]
    sin = sin_ref[...]
    lane = lax.broadcasted_iota(jnp.int32, cos.shape, 1)
    first_half = (lane % HEAD_DIM) < HEAD_DIM // 2
    q_scale = HEAD_DIM ** -0.5
    for out_ref, scale in ((q_ref, q_scale), (k_ref, None)):
        for j in range(dqk // LANES):
            r = _dot(h, w_ref[:, off:off + LANES])
            r = _rope(r, cos, sin, first_half)
            if scale is not None:
                r = r * scale
            out_ref[:, j * LANES:(j + 1) * LANES] = r.astype(BF16)
            off += LANES

    v_ref[...] = _dot(h, w_ref[:, off:off + dv]).astype(BF16)
    off += dv

    step = dg // N_BRANCHES
    for j in range(N_BRANCHES):
        r = _dot(h, w_ref[:, off:off + step])
        gate_ref[:, j * step:(j + 1) * step] = jax.nn.sigmoid(r).astype(BF16)
        off += step


def _in_proj(x, ada_l, g, w_in, cos, sin, *, ca, cb, tm):
    bsz, seq, d = x.shape
    dqk = N_HEADS * 2 * HEAD_DIM
    dv = N_HEADS * V_HEAD_DIM
    dg = N_BRANCHES * d
    assert w_in.shape == (d, 2 * ca + 3 * cb + 2 * dqk + dv + dg)
    row = lambda width: pl.BlockSpec((None, tm, width), lambda b, i: (b, i, 0))
    out = lambda width: jax.ShapeDtypeStruct((bsz, seq, width), BF16)
    widths = (ca, cb, cb, dqk, dqk, dv, dg)
    return pl.pallas_call(
        functools.partial(_inproj_kernel, ca=ca, cb=cb, dqk=dqk, dv=dv, dg=dg),
        grid=(bsz, seq // tm),
        in_specs=[
            row(d),
            pl.BlockSpec((None, N_ADA, d), lambda b, i: (b, 0, 0)),
            _resident((1, d)),
            _resident(w_in.shape),
            row(LANES),
            row(LANES),
        ],
        out_specs=[row(w) for w in widths],
        out_shape=[out(w) for w in widths],
        compiler_params=_params(2),
        name="in_proj",
    )(x, ada_l, g, w_in, cos, sin)


def _attn_kernel(q_ref, k_ref, v_ref, lq1_ref, lk1_ref, lq2_ref, lk2_ref, sg_ref, o_ref,
                 m_ref, l_ref, acc_ref, *, tq, tk, lambda_init):
    i = pl.program_id(2)
    q = q_ref[...]
    lane = lax.broadcasted_iota(jnp.int32, q.shape, 1)
    zero = jnp.zeros_like(q)
    q_comp = (jnp.where(lane < HEAD_DIM, q, zero), jnp.where(lane >= HEAD_DIM, q, zero))

    m_ref[...] = jnp.full(m_ref.shape, -jnp.inf, F32)
    l_ref[...] = jnp.zeros(l_ref.shape, F32)
    acc_ref[...] = jnp.zeros(acc_ref.shape, F32)

    def step(j, masked):
        start = pl.multiple_of(j * tk, tk)
        k = k_ref[pl.ds(start, tk), :]
        v = v_ref[pl.ds(start, tk), :]
        if masked:
            q_pos = i * tq + lax.broadcasted_iota(jnp.int32, (tq, tk), 0)
            k_pos = j * tk + lax.broadcasted_iota(jnp.int32, (tq, tk), 1)
            keep = q_pos >= k_pos
        for c in range(2):
            s = lax.dot_general(q_comp[c], k, (((1,), (1,)), ((), ())),
                                preferred_element_type=F32)
            if masked:
                s = jnp.where(keep, s, -jnp.inf)
            m_old = m_ref[c]
            m_new = jnp.maximum(m_old, jnp.max(s, axis=-1, keepdims=True))
            alpha = jnp.exp(m_old - m_new)
            p = jnp.exp(s - m_new)
            l_ref[c] = alpha * l_ref[c] + jnp.sum(p, axis=-1, keepdims=True)
            acc_ref[c] = alpha * acc_ref[c] + _dot(p.astype(BF16), v)
            m_ref[c] = m_new

    n_full = (i * tq) // tk
    n_diag = tq // tk

    def full_body(j, carry):
        step(j, False)
        return carry

    lax.fori_loop(0, n_full, full_body, 0)
    for d in range(n_diag):
        step(n_full + d, True)

    lam = (jnp.exp(jnp.sum(lq1_ref[...] * lk1_ref[...], axis=-1, keepdims=True))
           - jnp.exp(jnp.sum(lq2_ref[...] * lk2_ref[...], axis=-1, keepdims=True))
           + lambda_init)
    o = acc_ref[0] / l_ref[0] - lam * (acc_ref[1] / l_ref[1])
    ms = jnp.mean(o * o, axis=-1, keepdims=True)
    o = o * lax.rsqrt(ms + NORM_EPS) * sg_ref[...] * (1.0 - lambda_init)
    o_ref[...] = o.astype(BF16)


def _diff_attention(q, k, v, lq1, lk1, lq2, lk2, subln_g, *, lambda_init, tq, tk):
    bsz, seq, _ = q.shape
    assert tq % tk == 0
    head_blk = lambda rows, imap: pl.BlockSpec((None, rows, V_HEAD_DIM), imap)
    vec = lambda n: _resident((1, n))
    return pl.pallas_call(
        functools.partial(_attn_kernel, tq=tq, tk=tk, lambda_init=lambda_init),
        grid=(bsz, N_HEADS, seq // tq),
        in_specs=[
            head_blk(tq, lambda b, h, i: (b, i, h)),
            head_blk(seq, lambda b, h, i: (b, 0, h)),
            head_blk(seq, lambda b, h, i: (b, 0, h)),
            vec(HEAD_DIM), vec(HEAD_DIM), vec(HEAD_DIM), vec(HEAD_DIM),
            vec(V_HEAD_DIM),
        ],
        out_specs=head_blk(tq, lambda b, h, i: (b, i, h)),
        out_shape=jax.ShapeDtypeStruct((bsz, seq, N_HEADS * V_HEAD_DIM), BF16),
        scratch_shapes=[
            pltpu.VMEM((2, tq, 1), F32),
            pltpu.VMEM((2, tq, 1), F32),
            pltpu.VMEM((2, tq, V_HEAD_DIM), F32),
        ],
        compiler_params=_params(3),
        name="diff_attn",
    )(q, k, v, lq1, lk1, lq2, lk2, subln_g)


def _merge_kernel(x_ref, ada_ref, a_ref, ah_ref, u_ref, uh_ref, bg_ref, o_ref, gate_ref,
                  caw_ref, cab_ref, lng_ref, lnb_ref, wa_ref, cbw_ref, wb_ref, wc_ref, wo_ref,
                  out_ref, apad_ref, upad_ref, aact_ref, zb_ref,
                  *, tm, ka, kb, halo_a, halo_b, rows):
    i = pl.program_id(1)
    has_prev = jnp.where(i > 0, 1.0, 0.0)
    apad_ref[0:halo_a, :] = ah_ref[...].astype(F32) * has_prev
    apad_ref[halo_a:, :] = a_ref[...].astype(F32)
    upad_ref[0:halo_b, :] = uh_ref[...].astype(F32) * has_prev
    upad_ref[halo_b:, :] = u_ref[...].astype(F32)

    cab = cab_ref[...]
    lng = lng_ref[...]
    lnb = lnb_ref[...]
    for r0 in range(0, tm, rows):
        acc = jnp.zeros((rows, apad_ref.shape[1]), F32)
        for t in range(ka):
            lo = halo_a + r0 - (ka - 1) + t
            acc = acc + apad_ref[lo:lo + rows, :] * caw_ref[t:t + 1, :]
        acc = acc + cab
        mu = jnp.mean(acc, axis=-1, keepdims=True)
        cen = acc - mu
        var = jnp.mean(cen * cen, axis=-1, keepdims=True)
        y = cen * lax.rsqrt(var + NORM_EPS) * lng + lnb
        aact_ref[r0:r0 + rows, :] = (y * jax.nn.sigmoid(y)).astype(BF16)

        accb = jnp.zeros((rows, upad_ref.shape[1]), F32)
        for t in range(kb):
            lo = halo_b + r0 - (kb - 1) + t
            accb = accb + upad_ref[lo:lo + rows, :] * cbw_ref[t:t + 1, :]
        zb_ref[r0:r0 + rows, :] = (bg_ref[r0:r0 + rows, :].astype(F32) * accb).astype(BF16)

    d = out_ref.shape[-1]
    merged = gate_ref[:, 0:d].astype(F32) * _dot(aact_ref[...], wa_ref[...])
    merged = merged + gate_ref[:, d:2 * d].astype(F32) * _dot(zb_ref[...], wb_ref[...])
    merged = merged + gate_ref[:, 2 * d:3 * d].astype(F32) * _dot(o_ref[...], wc_ref[...])
    g_m = ada_ref[...][2:3]
    out_ref[...] = x_ref[...] + g_m * _dot(merged.astype(BF16), wo_ref[...])


def _merge(x, ada_l, a, u, bg, o, gate, conv_a_w, conv_a_b, ln_g, ln_b, w_a, conv_b_w,
           w_b, w_c, w_o, *, tm):
    bsz, seq, d = x.shape
    ka, ca = conv_a_w.shape
    kb, cb = conv_b_w.shape
    halo_a, halo_b = 32, 16
    assert ka - 1 <= halo_a and kb - 1 <= halo_b and tm % halo_a == 0
    row = lambda width: pl.BlockSpec((None, tm, width), lambda b, i: (b, i, 0))
    halo = lambda rows, width: pl.BlockSpec(
        (None, rows, width), lambda b, i: (b, jnp.maximum(i * (tm // rows) - 1, 0), 0))
    return pl.pallas_call(
        functools.partial(_merge_kernel, tm=tm, ka=ka, kb=kb, halo_a=halo_a, halo_b=halo_b,
                          rows=32),
        grid=(bsz, seq // tm),
        in_specs=[
            row(d),
            pl.BlockSpec((None, N_ADA, d), lambda b, i: (b, 0, 0)),
            row(ca), halo(halo_a, ca),
            row(cb), halo(halo_b, cb),
            row(cb), row(o.shape[-1]), row(gate.shape[-1]),
            _resident(conv_a_w.shape), _resident((1, ca)), _resident((1, ca)), _resident((1, ca)),
            _resident(w_a.shape), _resident(conv_b_w.shape), _resident(w_b.shape),
            _resident(w_c.shape), _resident(w_o.shape),
        ],
        out_specs=row(d),
        out_shape=jax.ShapeDtypeStruct(x.shape, F32),
        scratch_shapes=[
            pltpu.VMEM((halo_a + tm, ca), F32),
            pltpu.VMEM((halo_b + tm, cb), F32),
            pltpu.VMEM((tm, ca), BF16),
            pltpu.VMEM((tm, cb), BF16),
        ],
        compiler_params=_params(2),
        name="merge",
    )(x, ada_l, a, a, u, u, bg, o, gate, conv_a_w, conv_a_b.reshape(1, ca),
      ln_g.reshape(1, ca), ln_b.reshape(1, ca), w_a, conv_b_w, w_b, w_c, w_o)


def _mlp_kernel(x_ref, ada_ref, g_ref, w1_ref, w2_ref, fg_ref, out_ref, *, ff_chunk, final_norm):
    x = x_ref[...]
    ada = ada_ref[...]
    h = _rms_modulate(x, g_ref[...], ada[3:4], ada[4:5]).astype(BF16)
    acc = jnp.zeros(x.shape, F32)
    for c0 in range(0, w1_ref.shape[1], ff_chunk):
        f = jnp.maximum(_dot(h, w1_ref[:, c0:c0 + ff_chunk]), 0.0)
        acc = acc + _dot((f * f).astype(BF16), w2_ref[c0:c0 + ff_chunk, :])
    y = x + ada[5:6] * acc
    if final_norm:
        ms = jnp.mean(y * y, axis=-1, keepdims=True)
        y = y * lax.rsqrt(ms + NORM_EPS) * fg_ref[...]
    out_ref[...] = y


def _mlp(x, ada_l, g, w1, w2, final_g, *, tm, final_norm):
    bsz, seq, d = x.shape
    row = pl.BlockSpec((None, tm, d), lambda b, i: (b, i, 0))
    return pl.pallas_call(
        functools.partial(_mlp_kernel, ff_chunk=1024, final_norm=final_norm),
        grid=(bsz, seq // tm),
        in_specs=[
            row,
            pl.BlockSpec((None, N_ADA, d), lambda b, i: (b, 0, 0)),
            _resident((1, d)), _resident(w1.shape), _resident(w2.shape), _resident((1, d)),
        ],
        out_specs=row,
        out_shape=jax.ShapeDtypeStruct(x.shape, F32),
        compiler_params=_params(2),
        name="mlp",
    )(x, ada_l, g, w1, w2, final_g)


def kernel(x, c, positions, w_ada, b_ada, norm_mix_g, w_in, conv_a_w, conv_a_b, ln_a_g, ln_a_b, w_a_out, conv_b_w, w_b_out, lam_q1, lam_k1, lam_q2, lam_k2, subln_g, w_c_out, w_out, norm_mlp_g, w_ff1, w_ff2, final_g):
    bsz, seq, d = x.shape
    n_layers = w_in.shape[0]
    ca = conv_a_w.shape[2]
    cb = conv_b_w.shape[2]
    tm = min(seq, 512)
    tq = min(seq, 512)

    ada = _ada_all_layers(c, w_ada, b_ada).reshape(n_layers, bsz, N_ADA, d)
    cos, sin = _rope_tables(positions)
    w_in, w_a_out, w_b_out, w_c_out, w_out, w_ff1, w_ff2 = (
        w.astype(BF16) for w in (w_in, w_a_out, w_b_out, w_c_out, w_out, w_ff1, w_ff2))
    vec = lambda p, l: p[l].reshape(1, -1)

    for l in range(n_layers):
        lambda_init = 0.8 - 0.6 * math.exp(-0.3 * l)
        a, bg, u, q, k, v, gate = _in_proj(
            x, ada[l], vec(norm_mix_g, l), w_in[l], cos, sin, ca=ca, cb=cb, tm=tm)
        o = _diff_attention(
            q, k, v, vec(lam_q1, l), vec(lam_k1, l), vec(lam_q2, l), vec(lam_k2, l),
            vec(subln_g, l), lambda_init=lambda_init, tq=tq, tk=tq)
        x = _merge(x, ada[l], a, u, bg, o, gate, conv_a_w[l], conv_a_b[l], ln_a_g[l],
                   ln_a_b[l], w_a_out[l], conv_b_w[l], w_b_out[l], w_c_out[l], w_out[l], tm=tm)
        x = _mlp(x, ada[l], vec(norm_mlp_g, l), w_ff1[l], w_ff2[l], final_g.reshape(1, d),
                 tm=tm, final_norm=(l == n_layers - 1))
    return x
```

```python
import functools
import math

import jax
import jax.numpy as jnp
from jax import lax
from jax.experimental import pallas as pl
from jax.experimental.pallas import tpu as pltpu

N_HEADS = 8
HEAD_DIM = 64
V_HEAD_DIM = 2 * HEAD_DIM
N_ADA = 6
N_BRANCHES = 3
NORM_EPS = 1e-6
ROPE_THETA = 10000.0

LANES = 128
VMEM_LIMIT_BYTES = 56 * 1024 * 1024

F32 = jnp.float32
BF16 = jnp.bfloat16


def _params(n_axes):
    return pltpu.CompilerParams(
        dimension_semantics=("arbitrary",) * n_axes,
        vmem_limit_bytes=VMEM_LIMIT_BYTES)


def _resident(shape):
    nd = len(shape)
    return pl.BlockSpec(shape, lambda *_: (0,) * nd, pipeline_mode=pl.Buffered(1))


def _dot(a, b):
    return jnp.dot(a, b, preferred_element_type=F32)


def _ada_kernel(c_ref, w_ref, b_ref, o_ref):
    c = c_ref[...]
    c_act = c * jax.nn.sigmoid(c)
    o_ref[...] = _dot(c_act.astype(BF16), w_ref[...].astype(BF16)) + b_ref[...]


def _ada_all_layers(c, w_ada, b_ada):
    n_layers, d, n_out = w_ada.shape
    bsz = c.shape[0]
    tn = n_out // 4
    return pl.pallas_call(
        _ada_kernel,
        grid=(n_layers, n_out // tn),
        in_specs=[
            pl.BlockSpec((bsz, d), lambda l, j: (0, 0)),
            pl.BlockSpec((None, d, tn), lambda l, j: (l, 0, j)),
            pl.BlockSpec((None, 1, tn), lambda l, j: (l, 0, j)),
        ],
        out_specs=pl.BlockSpec((None, bsz, tn), lambda l, j: (l, 0, j)),
        out_shape=jax.ShapeDtypeStruct((n_layers, bsz, n_out), F32),
        compiler_params=_params(2),
        name="ada",
    )(c, w_ada, b_ada.reshape(n_layers, 1, n_out))


def _rope_kernel(pos_ref, invf_ref, cos_ref, sin_ref):
    ang = pos_ref[...].astype(F32) * invf_ref[...]
    lane = lax.broadcasted_iota(jnp.int32, ang.shape, 1)
    sign = jnp.where((lane % HEAD_DIM) < HEAD_DIM // 2, -1.0, 1.0)
    cos_ref[...] = jnp.cos(ang)
    sin_ref[...] = jnp.sin(ang) * sign


def _rope_tables(positions):
    bsz, seq = positions.shape
    inv_freq = ROPE_THETA ** (-jnp.arange(0, HEAD_DIM, 2, dtype=F32) / HEAD_DIM)
    invf = jnp.tile(inv_freq, LANES // (HEAD_DIM // 2)).reshape(1, LANES)
    ts = min(seq, 2048)
    table = jax.ShapeDtypeStruct((bsz, seq, LANES), F32)
    return pl.pallas_call(
        _rope_kernel,
        grid=(bsz, seq // ts),
        in_specs=[
            pl.BlockSpec((None, ts, 1), lambda b, i: (b, i, 0)),
            pl.BlockSpec((1, LANES), lambda b, i: (0, 0)),
        ],
        out_specs=[pl.BlockSpec((None, ts, LANES), lambda b, i: (b, i, 0))] * 2,
        out_shape=[table, table],
        compiler_params=_params(2),
        name="rope_tables",
    )(positions.reshape(bsz, seq, 1), invf)


def _rms_modulate(x, g, shift, scale):
    ms = jnp.mean(x * x, axis=-1, keepdims=True)
    y = x * lax.rsqrt(ms + NORM_EPS) * g
    return y * (1.0 + scale) + shift


def _rope(t, cos, sin_signed, first_half):
    swapped = jnp.where(first_half, pltpu.roll(t, LANES - HEAD_DIM // 2, 1),
                        pltpu.roll(t, HEAD_DIM // 2, 1))
    return t * cos + swapped * sin_signed


def _inproj_kernel(x_ref, ada_ref, g_ref, w_ref, cos_ref, sin_ref,
                   a_ref, bg_ref, u_ref, q_ref, k_ref, v_ref, gate_ref,
                   *, ca, cb, dqk, dv, dg):
    ada = ada_ref[...]
    h = _rms_modulate(x_ref[...], g_ref[...], ada[0:1], ada[1:2]).astype(BF16)

    off = 0
    r = _dot(h, w_ref[:, off:off + 2 * ca])
    a_ref[...] = (r[:, :ca] * jax.nn.sigmoid(r[:, ca:])).astype(BF16)
    off += 2 * ca

    r = _dot(h, w_ref[:, off:off + 3 * cb])
    bg_ref[...] = r[:, :cb].astype(BF16)
    u_ref[...] = (r[:, cb:2 * cb] * r[:, 2 * cb:]).astype(BF16)
    off += 3 * cb

    cos = cos_ref[...]
    sin = sin_ref[...]
    lane = lax.broadcasted_iota(jnp.int32, cos.shape, 1)
    first_half = (lane % HEAD_DIM) < HEAD_DIM // 2
    q_scale = HEAD_DIM ** -0.5
    for out_ref, scale in ((q_ref, q_scale), (k_ref, None)):
        for j in range(dqk // LANES):
            r = _dot(h, w_ref[:, off:off + LANES])
            r = _rope(r, cos, sin, first_half)
            if scale is not None:
                r = r * scale
            out_ref[:, j * LANES:(j + 1) * LANES] = r.astype(BF16)
            off += LANES

    v_ref[...] = _dot(h, w_ref[:, off:off + dv]).astype(BF16)
    off += dv

    step = dg // N_BRANCHES
    for j in range(N_BRANCHES):
        r = _dot(h, w_ref[:, off:off + step])
        gate_ref[:, j * step:(j + 1) * step] = jax.nn.sigmoid(r).astype(BF16)
        off += step


def _in_proj(x, ada_l, g, w_in, cos, sin, *, ca, cb, tm):
    bsz, seq, d = x.shape
    dqk = N_HEADS * 2 * HEAD_DIM
    dv = N_HEADS * V_HEAD_DIM
    dg = N_BRANCHES * d
    assert w_in.shape == (d, 2 * ca + 3 * cb + 2 * dqk + dv + dg)
    row = lambda width: pl.BlockSpec((None, tm, width), lambda b, i: (b, i, 0))
    out = lambda width: jax.ShapeDtypeStruct((bsz, seq, width), BF16)
    widths = (ca, cb, cb, dqk, dqk, dv, dg)
    return pl.pallas_call(
        functools.partial(_inproj_kernel, ca=ca, cb=cb, dqk=dqk, dv=dv, dg=dg),
        grid=(bsz, seq // tm),
        in_specs=[
            row(d),
            pl.BlockSpec((None, N_ADA, d), lambda b, i: (b, 0, 0)),
            _resident((1, d)),
            _resident(w_in.shape),
            row(LANES),
            row(LANES),
        ],
        out_specs=[row(w) for w in widths],
        out_shape=[out(w) for w in widths],
        compiler_params=_params(2),
        name="in_proj",
    )(x, ada_l, g, w_in, cos, sin)


def _attn_kernel(q_ref, k_ref, v_ref, lq1_ref, lk1_ref, lq2_ref, lk2_ref, sg_ref, o_ref,
                 m_ref, l_ref, acc_ref, *, tq, tk, lambda_init):
    i = pl.program_id(2)
    q = q_ref[...]
    lane = lax.broadcasted_iota(jnp.int32, q.shape, 1)
    zero = jnp.zeros_like(q)
    q_comp = (jnp.where(lane < HEAD_DIM, q, zero), jnp.where(lane >= HEAD_DIM, q, zero))

    m_ref[...] = jnp.full(m_ref.shape, -jnp.inf, F32)
    l_ref[...] = jnp.zeros(l_ref.shape, F32)
    acc_ref[...] = jnp.zeros(acc_ref.shape, F32)

    def step(j, masked):
        start = pl.multiple_of(j * tk, tk)
        k = k_ref[pl.ds(start, tk), :]
        v = v_ref[pl.ds(start, tk), :]
        if masked:
            q_pos = i * tq + lax.broadcasted_iota(jnp.int32, (tq, tk), 0)
            k_pos = j * tk + lax.broadcasted_iota(jnp.int32, (tq, tk), 1)
            keep = q_pos >= k_pos
        for c in range(2):
            s = lax.dot_general(q_comp[c], k, (((1,), (1,)), ((), ())),
                                preferred_element_type=F32)
            if masked:
                s = jnp.where(keep, s, -jnp.inf)
            tiles = [s[:, t * LANES:(t + 1) * LANES] for t in range(tk // LANES)]
            m_old = m_ref[c]
            m_new = jnp.maximum(
                m_old, jnp.max(functools.reduce(jnp.maximum, tiles), axis=-1, keepdims=True))
            alpha = jnp.exp(m_old - m_new)
            p = [jnp.exp(t - m_new) for t in tiles]
            l_ref[c] = alpha * l_ref[c] + functools.reduce(jnp.add, p)
            pv = _dot(jnp.concatenate(p, axis=-1).astype(BF16), v)
            acc_ref[c] = alpha * acc_ref[c] + pv
            m_ref[c] = m_new

    n_full = (i * tq) // tk
    n_diag = tq // tk

    def full_body(j, carry):
        step(j, False)
        return carry

    lax.fori_loop(0, n_full, full_body, 0)
    for d in range(n_diag):
        step(n_full + d, True)

    lam = (jnp.exp(jnp.sum(lq1_ref[...] * lk1_ref[...], axis=-1, keepdims=True))
           - jnp.exp(jnp.sum(lq2_ref[...] * lk2_ref[...], axis=-1, keepdims=True))
           + lambda_init)
    l0 = jnp.sum(l_ref[0], axis=-1, keepdims=True)
    l1 = jnp.sum(l_ref[1], axis=-1, keepdims=True)
    o = acc_ref[0] / l0 - lam * (acc_ref[1] / l1)
    ms = jnp.mean(o * o, axis=-1, keepdims=True)
    o = o * lax.rsqrt(ms + NORM_EPS) * sg_ref[...] * (1.0 - lambda_init)
    o_ref[...] = o.astype(BF16)


def _diff_attention(q, k, v, lq1, lk1, lq2, lk2, subln_g, *, lambda_init, tq, tk):
    bsz, seq, _ = q.shape
    assert tq % tk == 0
    head_blk = lambda rows, imap: pl.BlockSpec((None, rows, V_HEAD_DIM), imap)
    vec = lambda n: _resident((1, n))
    return pl.pallas_call(
        functools.partial(_attn_kernel, tq=tq, tk=tk, lambda_init=lambda_init),
        grid=(bsz, N_HEADS, seq // tq),
        in_specs=[
            head_blk(tq, lambda b, h, i: (b, i, h)),
            head_blk(seq, lambda b, h, i: (b, 0, h)),
            head_blk(seq, lambda b, h, i: (b, 0, h)),
            vec(HEAD_DIM), vec(HEAD_DIM), vec(HEAD_DIM), vec(HEAD_DIM),
            vec(V_HEAD_DIM),
        ],
        out_specs=head_blk(tq, lambda b, h, i: (b, i, h)),
        out_shape=jax.ShapeDtypeStruct((bsz, seq, N_HEADS * V_HEAD_DIM), BF16),
        scratch_shapes=[
            pltpu.VMEM((2, tq, LANES), F32),
            pltpu.VMEM((2, tq, LANES), F32),
            pltpu.VMEM((2, tq, V_HEAD_DIM), F32),
        ],
        compiler_params=_params(3),
        name="diff_attn",
    )(q, k, v, lq1, lk1, lq2, lk2, subln_g)


def _merge_kernel(x_ref, ada_ref, a_ref, ah_ref, u_ref, uh_ref, bg_ref, o_ref, gate_ref,
                  caw_ref, cab_ref, lng_ref, lnb_ref, wa_ref, cbw_ref, wb_ref, wc_ref, wo_ref,
                  out_ref, apad_ref, upad_ref, aact_ref, zb_ref,
                  *, tm, ka, kb, halo_a, halo_b, rows):
    i = pl.program_id(1)
    has_prev = jnp.where(i > 0, 1.0, 0.0)
    apad_ref[0:halo_a, :] = ah_ref[...].astype(F32) * has_prev
    apad_ref[halo_a:, :] = a_ref[...].astype(F32)
    upad_ref[0:halo_b, :] = uh_ref[...].astype(F32) * has_prev
    upad_ref[halo_b:, :] = u_ref[...].astype(F32)

    cab = cab_ref[...]
    lng = lng_ref[...]
    lnb = lnb_ref[...]
    for r0 in range(0, tm, rows):
        acc = jnp.zeros((rows, apad_ref.shape[1]), F32)
        for t in range(ka):
            lo = halo_a + r0 - (ka - 1) + t
            acc = acc + apad_ref[lo:lo + rows, :] * caw_ref[t:t + 1, :]
        acc = acc + cab
        mu = jnp.mean(acc, axis=-1, keepdims=True)
        cen = acc - mu
        var = jnp.mean(cen * cen, axis=-1, keepdims=True)
        y = cen * lax.rsqrt(var + NORM_EPS) * lng + lnb
        aact_ref[r0:r0 + rows, :] = (y * jax.nn.sigmoid(y)).astype(BF16)

        accb = jnp.zeros((rows, upad_ref.shape[1]), F32)
        for t in range(kb):
            lo = halo_b + r0 - (kb - 1) + t
            accb = accb + upad_ref[lo:lo + rows, :] * cbw_ref[t:t + 1, :]
        zb_ref[r0:r0 + rows, :] = (bg_ref[r0:r0 + rows, :].astype(F32) * accb).astype(BF16)

    d = out_ref.shape[-1]
    merged = gate_ref[:, 0:d].astype(F32) * _dot(aact_ref[...], wa_ref[...])
    merged = merged + gate_ref[:, d:2 * d].astype(F32) * _dot(zb_ref[...], wb_ref[...])
    merged = merged + gate_ref[:, 2 * d:3 * d].astype(F32) * _dot(o_ref[...], wc_ref[...])
    g_m = ada_ref[...][2:3]
    out_ref[...] = x_ref[...] + g_m * _dot(merged.astype(BF16), wo_ref[...])


def _merge(x, ada_l, a, u, bg, o, gate, conv_a_w, conv_a_b, ln_g, ln_b, w_a, conv_b_w,
           w_b, w_c, w_o, *, tm):
    bsz, seq, d = x.shape
    ka, ca = conv_a_w.shape
    kb, cb = conv_b_w.shape
    halo_a, halo_b = 32, 16
    assert ka - 1 <= halo_a and kb - 1 <= halo_b and tm % halo_a == 0
    row = lambda width: pl.BlockSpec((None, tm, width), lambda b, i: (b, i, 0))
    halo = lambda rows, width: pl.BlockSpec(
        (None, rows, width), lambda b, i: (b, jnp.maximum(i * (tm // rows) - 1, 0), 0))
    return pl.pallas_call(
        functools.partial(_merge_kernel, tm=tm, ka=ka, kb=kb, halo_a=halo_a, halo_b=halo_b,
                          rows=32),
        grid=(bsz, seq // tm),
        in_specs=[
            row(d),
            pl.BlockSpec((None, N_ADA, d), lambda b, i: (b, 0, 0)),
            row(ca), halo(halo_a, ca),
            row(cb), halo(halo_b, cb),
            row(cb), row(o.shape[-1]), row(gate.shape[-1]),
            _resident(conv_a_w.shape), _resident((1, ca)), _resident((1, ca)), _resident((1, ca)),
            _resident(w_a.shape), _resident(conv_b_w.shape), _resident(w_b.shape),
            _resident(w_c.shape), _resident(w_o.shape),
        ],
        out_specs=row(d),
        out_shape=jax.ShapeDtypeStruct(x.shape, F32),
        scratch_shapes=[
            pltpu.VMEM((halo_a + tm, ca), F32),
            pltpu.VMEM((halo_b + tm, cb), F32),
            pltpu.VMEM((tm, ca), BF16),
            pltpu.VMEM((tm, cb), BF16),
        ],
        compiler_params=_params(2),
        name="merge",
    )(x, ada_l, a, a, u, u, bg, o, gate, conv_a_w, conv_a_b.reshape(1, ca),
      ln_g.reshape(1, ca), ln_b.reshape(1, ca), w_a, conv_b_w, w_b, w_c, w_o)


def _mlp_kernel(x_ref, ada_ref, g_ref, w1_ref, w2_ref, fg_ref, out_ref, *, ff_chunk, final_norm):
    x = x_ref[...]
    ada = ada_ref[...]
    h = _rms_modulate(x, g_ref[...], ada[3:4], ada[4:5]).astype(BF16)
    acc = jnp.zeros(x.shape, F32)
    for c0 in range(0, w1_ref.shape[1], ff_chunk):
        f = jnp.maximum(_dot(h, w1_ref[:, c0:c0 + ff_chunk]), 0.0)
        acc = acc + _dot((f * f).astype(BF16), w2_ref[c0:c0 + ff_chunk, :])
    y = x + ada[5:6] * acc
    if final_norm:
        ms = jnp.mean(y * y, axis=-1, keepdims=True)
        y = y * lax.rsqrt(ms + NORM_EPS) * fg_ref[...]
    out_ref[...] = y


def _mlp(x, ada_l, g, w1, w2, final_g, *, tm, final_norm):
    bsz, seq, d = x.shape
    row = pl.BlockSpec((None, tm, d), lambda b, i: (b, i, 0))
    return pl.pallas_call(
        functools.partial(_mlp_kernel, ff_chunk=1024, final_norm=final_norm),
        grid=(bsz, seq // tm),
        in_specs=[
            row,
            pl.BlockSpec((None, N_ADA, d), lambda b, i: (b, 0, 0)),
            _resident((1, d)), _resident(w1.shape), _resident(w2.shape), _resident((1, d)),
        ],
        out_specs=row,
        out_shape=jax.ShapeDtypeStruct(x.shape, F32),
        compiler_params=_params(2),
        name="mlp",
    )(x, ada_l, g, w1, w2, final_g)


def kernel(x, c, positions, w_ada, b_ada, norm_mix_g, w_in, conv_a_w, conv_a_b, ln_a_g, ln_a_b, w_a_out, conv_b_w, w_b_out, lam_q1, lam_k1, lam_q2, lam_k2, subln_g, w_c_out, w_out, norm_mlp_g, w_ff1, w_ff2, final_g):
    bsz, seq, d = x.shape
    n_layers = w_in.shape[0]
    ca = conv_a_w.shape[2]
    cb = conv_b_w.shape[2]
    tm = min(seq, 512)
    tq = min(seq, 512)

    ada = _ada_all_layers(c, w_ada, b_ada).reshape(n_layers, bsz, N_ADA, d)
    cos, sin = _rope_tables(positions)
    w_in, w_a_out, w_b_out, w_c_out, w_out, w_ff1, w_ff2 = (
        w.astype(BF16) for w in (w_in, w_a_out, w_b_out, w_c_out, w_out, w_ff1, w_ff2))
    vec = lambda p, l: p[l].reshape(1, -1)

    for l in range(n_layers):
        lambda_init = 0.8 - 0.6 * math.exp(-0.3 * l)
        a, bg, u, q, k, v, gate = _in_proj(
            x, ada[l], vec(norm_mix_g, l), w_in[l], cos, sin, ca=ca, cb=cb, tm=tm)
        o = _diff_attention(
            q, k, v, vec(lam_q1, l), vec(lam_k1, l), vec(lam_q2, l), vec(lam_k2, l),
            vec(subln_g, l), lambda_init=lambda_init, tq=tq, tk=tq)
        x = _merge(x, ada[l], a, u, bg, o, gate, conv_a_w[l], conv_a_b[l], ln_a_g[l],
                   ln_a_b[l], w_a_out[l], conv_b_w[l], w_b_out[l], w_c_out[l], w_out[l], tm=tm)
        x = _mlp(x, ada[l], vec(norm_mlp_g, l), w_ff1[l], w_ff2[l], final_g.reshape(1, d),
                 tm=tm, final_norm=(l == n_layers - 1))
    return x
```
